```python
import math
import jax, jax.numpy as jnp
from jax import lax
import numpy as np

D_MODEL = 1024
BATCH = 4
SEQ = 8192
DEPTH = 1
DEC_BATCH = 128
DEC_SEQ = 8
PAST_LEN = 16384
PAGE_SIZE = 128

N_HEADS = 8
QK_NOPE = 64
QK_ROPE = 32
V_DIM = 64
Q_LORA = 256
KV_LORA = 256
MLA_WIDTH = N_HEADS * V_DIM
ATTN_SCALE = 1.0 / math.sqrt(QK_NOPE + QK_ROPE)
ROPE_BASE = 10000.0
Q_BLOCK = 128
LRU_WIDTH = 512
LRU_BLOCKS = 8
LRU_BLOCK = LRU_WIDTH // LRU_BLOCKS
CONV_W = 4
LRU_C = 8.0
EPS = 1e-6
IN_WIDTHS = (Q_LORA, KV_LORA, QK_ROPE, MLA_WIDTH, LRU_WIDTH, LRU_WIDTH, D_MODEL, D_MODEL)
IN_COLS = sum(IN_WIDTHS)

kernel_name = 'hybrid_mla_rglru_decode_step'


def rmsnorm(x, w):
    xf = x.astype(jnp.float32)
    y = xf * lax.rsqrt(jnp.mean(xf * xf, axis=-1, keepdims=True) + EPS)
    return y.astype(x.dtype) * w


def rope_tables(pos):
    half = QK_ROPE // 2
    inv = jnp.power(ROPE_BASE, -jnp.arange(half, dtype=jnp.float32) / half)
    ang = pos.astype(jnp.float32)[:, None] * inv[None, :]
    return jnp.cos(ang), jnp.sin(ang)


def rope(x, cos, sin):
    x1, x2 = jnp.split(x.astype(jnp.float32), 2, axis=-1)
    return jnp.concatenate([x1 * cos - x2 * sin, x2 * cos + x1 * sin], axis=-1).astype(x.dtype)


def split_in(proj):
    idx = [int(v) for v in np.cumsum(IN_WIDTHS)[:-1]]
    return jnp.split(proj, idx, axis=-1)


def causal_conv(u, buf, w, b):
    s = u.shape[1]
    ext = jnp.concatenate([buf, u], axis=1)
    y = b + ext[:, 0:s] * w[0]
    for k in range(1, CONV_W):
        y = y + ext[:, k:k + s] * w[k]
    return y, ext[:, ext.shape[1] - (CONV_W - 1):]


def lru_combine(left, right):
    a1, b1 = left
    a2, b2 = right
    return a1 * a2, a2 * b1 + b2


def rg_lru(u, h0, wa, ba, wx, bx, lam):
    bsz, s, _ = u.shape
    ub = u.reshape(bsz, s, LRU_BLOCKS, LRU_BLOCK)
    r = jax.nn.sigmoid((jnp.einsum('bsnd,nde->bsne', ub, wa).reshape(bsz, s, LRU_WIDTH) + ba).astype(jnp.float32))
    i = jax.nn.sigmoid((jnp.einsum('bsnd,nde->bsne', ub, wx).reshape(bsz, s, LRU_WIDTH) + bx).astype(jnp.float32))
    log_a = -LRU_C * r * jax.nn.softplus(-lam.astype(jnp.float32))
    a = jnp.exp(log_a)
    b = jnp.sqrt(-jnp.expm1(2.0 * log_a)) * i * u.astype(jnp.float32)
    b = b.at[:, 0].add(a[:, 0] * h0.astype(jnp.float32))
    _, h = lax.associative_scan(lru_combine, (a, b), axis=1)
    return h.astype(u.dtype), h[:, -1].astype(h0.dtype)


def setup_inputs(seed: int = 0) -> dict:
    key = jax.random.key(seed)
    ks = jax.random.split(key, 32)
    f32 = jnp.float32
    n_pages = PAST_LEN // PAGE_SIZE
    n_used = DEC_BATCH * n_pages
    n_pool = n_used + n_used // 4
    nrm = lambda k, shape, sc: jax.random.normal(k, shape, f32) * sc
    page_table = jax.random.permutation(ks[0], n_pool)[:n_used].reshape(DEC_BATCH, n_pages).astype(jnp.int32)
    s8 = jax.random.uniform(ks[1], (DEPTH, LRU_WIDTH), f32, 0.9, 0.999) ** (1.0 / LRU_C)
    lru_lambda = jnp.log(s8) - jnp.log1p(-s8)
    return {
        'x_prompt': nrm(ks[2], (BATCH, SEQ, D_MODEL), 1.0),
        'x_sample': nrm(ks[3], (DEC_BATCH, DEC_SEQ, D_MODEL), 1.0),
        'c_prompt': nrm(ks[4], (BATCH, D_MODEL), 1.0),
        'c_sample': nrm(ks[5], (DEC_BATCH, D_MODEL), 1.0),
        'cache_ckv': nrm(ks[6], (DEPTH, n_pool, PAGE_SIZE, KV_LORA), 1.0),
        'cache_kpe': nrm(ks[7], (DEPTH, n_pool, PAGE_SIZE, QK_ROPE), 1.0),
        'state_conv': nrm(ks[8], (DEPTH, DEC_BATCH, CONV_W - 1, LRU_WIDTH), 1.0),
        'state_lru': nrm(ks[9], (DEPTH, DEC_BATCH, LRU_WIDTH), 0.5),
        'page_table': page_table,
        'ada_w': nrm(ks[10], (DEPTH, D_MODEL, 3 * D_MODEL), D_MODEL ** -0.5),
        'ada_b': nrm(ks[11], (DEPTH, 3 * D_MODEL), 0.02),
        'norm_w': 1.0 + nrm(ks[12], (DEPTH, D_MODEL), 0.05),
        'w_in': nrm(ks[13], (DEPTH, D_MODEL, IN_COLS), D_MODEL ** -0.5),
        'q_norm_w': 1.0 + nrm(ks[14], (DEPTH, Q_LORA), 0.05),
        'kv_norm_w': 1.0 + nrm(ks[15], (DEPTH, KV_LORA), 0.05),
        'w_uq': nrm(ks[16], (DEPTH, Q_LORA, N_HEADS, QK_NOPE + QK_ROPE), Q_LORA ** -0.5),
        'w_uk': nrm(ks[17], (DEPTH, KV_LORA, N_HEADS, QK_NOPE), KV_LORA ** -0.5),
        'w_uv': nrm(ks[18], (DEPTH, KV_LORA, N_HEADS, V_DIM), KV_LORA ** -0.5),
        'conv_w': nrm(ks[19], (DEPTH, CONV_W, LRU_WIDTH), CONV_W ** -0.5),
        'conv_b': nrm(ks[20], (DEPTH, LRU_WIDTH), 0.02),
        'lru_wa': nrm(ks[21], (DEPTH, LRU_BLOCKS, LRU_BLOCK, LRU_BLOCK), LRU_BLOCK ** -0.5),
        'lru_ba': nrm(ks[22], (DEPTH, LRU_WIDTH), 0.02),
        'lru_wx': nrm(ks[23], (DEPTH, LRU_BLOCKS, LRU_BLOCK, LRU_BLOCK), LRU_BLOCK ** -0.5),
        'lru_bx': nrm(ks[24], (DEPTH, LRU_WIDTH), 0.02),
        'lru_lambda': lru_lambda,
        'w_oa': nrm(ks[25], (DEPTH, MLA_WIDTH, D_MODEL), MLA_WIDTH ** -0.5),
        'w_ob': nrm(ks[26], (DEPTH, LRU_WIDTH, D_MODEL), LRU_WIDTH ** -0.5),
        'w_out': nrm(ks[27], (DEPTH, D_MODEL, D_MODEL), D_MODEL ** -0.5),
        'final_norm_w': 1.0 + nrm(ks[28], (D_MODEL,), 0.05),
    }


def reference(x_prompt, x_sample, c_prompt, c_sample, cache_ckv, cache_kpe, state_conv, state_lru, page_table,
              ada_w, ada_b, norm_w, w_in, q_norm_w, kv_norm_w, w_uq, w_uk, w_uv, conv_w, conv_b,
              lru_wa, lru_ba, lru_wx, lru_bx, lru_lambda, w_oa, w_ob, w_out, final_norm_w):
    f32 = jnp.float32

    def attend_prompt(l, q_nope, q_pe, c_kv, k_pe):
        bsz, s = q_nope.shape[0], q_nope.shape[1]
        k_nope = jnp.einsum('bsr,rhd->bshd', c_kv, w_uk[l])
        v = jnp.einsum('bsr,rhd->bshd', c_kv, w_uv[l])
        kpos = jnp.arange(s)

        def block(i):
            start = i * Q_BLOCK
            qn = lax.dynamic_slice_in_dim(q_nope, start, Q_BLOCK, axis=1)
            qp = lax.dynamic_slice_in_dim(q_pe, start, Q_BLOCK, axis=1)
            sc = jnp.einsum('bqhd,bkhd->bhqk', qn, k_nope) + jnp.einsum('bqhd,bkd->bhqk', qp, k_pe)
            sc = sc.astype(f32) * ATTN_SCALE
            qpos = start + jnp.arange(Q_BLOCK)
            sc = jnp.where(kpos[None, :] <= qpos[:, None], sc, -jnp.inf)
            p = jax.nn.softmax(sc, axis=-1).astype(v.dtype)
            return jnp.einsum('bhqk,bkhd->bqhd', p, v)

        o = lax.map(block, jnp.arange(s // Q_BLOCK))
        return jnp.moveaxis(o, 0, 1).reshape(bsz, s, MLA_WIDTH)

    def attend_sample(l, q_nope, q_pe, c_kv, k_pe):
        dbsz, t = q_nope.shape[0], q_nope.shape[1]
        past = page_table.shape[1] * PAGE_SIZE
        ckv_past = cache_ckv[l][page_table].reshape(dbsz, past, KV_LORA)
        kpe_past = cache_kpe[l][page_table].reshape(dbsz, past, QK_ROPE)
        q_abs = jnp.einsum('bthd,rhd->bthr', q_nope, w_uk[l])
        s_past = jnp.einsum('bthr,bpr->bhtp', q_abs, ckv_past) + jnp.einsum('bthd,bpd->bhtp', q_pe, kpe_past)
        s_new = jnp.einsum('bthr,bsr->bhts', q_abs, c_kv) + jnp.einsum('bthd,bsd->bhts', q_pe, k_pe)
        tpos = jnp.arange(t)
        s_new = jnp.where(tpos[None, :] <= tpos[:, None], s_new.astype(f32), -jnp.inf)
        sc = jnp.concatenate([s_past.astype(f32), s_new], axis=-1) * ATTN_SCALE
        p = jax.nn.softmax(sc, axis=-1).astype(c_kv.dtype)
        o_lat = jnp.einsum('bhtp,bpr->bthr', p[..., :past], ckv_past) + jnp.einsum('bhts,bsr->bthr', p[..., past:], c_kv)
        return jnp.einsum('bthr,rhd->bthd', o_lat, w_uv[l]).reshape(dbsz, t, MLA_WIDTH)

    def layer(l, x, c, pos, conv_buf, h0, attend):
        shift, scale, gate = jnp.split(jax.nn.silu(c) @ ada_w[l] + ada_b[l], 3, axis=-1)
        h = rmsnorm(x, norm_w[l]) * (1.0 + scale[:, None, :]) + shift[:, None, :]
        q_lat, kv_lat, k_pe, g_a, u, g_b, m_a, m_b = split_in(h @ w_in[l])
        cos, sin = rope_tables(pos)
        q = jnp.einsum('bsr,rhd->bshd', rmsnorm(q_lat, q_norm_w[l]), w_uq[l])
        q_nope = q[..., :QK_NOPE]
        q_pe = rope(q[..., QK_NOPE:], cos[:, None, :], sin[:, None, :])
        c_kv = rmsnorm(kv_lat, kv_norm_w[l])
        k_pe = rope(k_pe, cos, sin)
        attn = attend(l, q_nope, q_pe, c_kv, k_pe)
        y_a = (attn * jax.nn.silu(g_a)) @ w_oa[l]
        u_c, new_conv = causal_conv(u, conv_buf, conv_w[l], conv_b[l])
        lru, h_last = rg_lru(u_c, h0, lru_wa[l], lru_ba[l], lru_wx[l], lru_bx[l], lru_lambda[l])
        y_b = (lru * jax.nn.silu(g_b)) @ w_ob[l]
        merged = jax.nn.sigmoid(m_a) * y_a + jax.nn.sigmoid(m_b) * y_b
        x = x + gate[:, None, :] * (merged @ w_out[l])
        return x, c_kv, k_pe, new_conv, h_last

    bsz, s = x_prompt.shape[0], x_prompt.shape[1]
    dbsz, t = x_sample.shape[0], x_sample.shape[1]
    pos_prompt = jnp.arange(s)
    pos_sample = page_table.shape[1] * PAGE_SIZE + jnp.arange(t)

    xp, xs = x_prompt, x_sample
    ckv_p, kpe_p, conv_p, lru_p = [], [], [], []
    ckv_s, kpe_s, conv_s, lru_s = [], [], [], []
    for l in range(DEPTH):
        zeros_conv = jnp.zeros((bsz, CONV_W - 1, LRU_WIDTH), x_prompt.dtype)
        zeros_h = jnp.zeros((bsz, LRU_WIDTH), state_lru.dtype)
        xp, a1, a2, a3, a4 = layer(l, xp, c_prompt, pos_prompt, zeros_conv, zeros_h, attend_prompt)
        ckv_p.append(a1); kpe_p.append(a2); conv_p.append(a3); lru_p.append(a4)
        xs, b1, b2, b3, b4 = layer(l, xs, c_sample, pos_sample, state_conv[l], state_lru[l], attend_sample)
        ckv_s.append(b1); kpe_s.append(b2); conv_s.append(b3); lru_s.append(b4)

    y_prompt = rmsnorm(xp, final_norm_w)
    y_sample = rmsnorm(xs, final_norm_w)
    new_ckv_prompt = jnp.stack(ckv_p)
    new_kpe_prompt = jnp.stack(kpe_p)
    new_conv_prompt = jnp.stack(conv_p)
    new_lru_prompt = jnp.stack(lru_p)
    new_ckv_sample = jnp.stack(ckv_s)
    new_kpe_sample = jnp.stack(kpe_s)
    new_conv_sample = jnp.stack(conv_s)
    new_lru_sample = jnp.stack(lru_s)
    return (y_prompt, y_sample, new_ckv_prompt, new_kpe_prompt, new_conv_prompt, new_lru_prompt,
            new_ckv_sample, new_kpe_sample, new_conv_sample, new_lru_sample)
```

```python
import functools
import math

import numpy as np
import jax
import jax.numpy as jnp
from jax import lax
from jax.experimental import pallas as pl
from jax.experimental.pallas import tpu as pltpu

_EPS = 1e-6
_LRU_C = 8.0
_ROPE_BASE = 10000.0
_LOG2E = 1.4426950408889634
_NEG = -1e30
_LANES = 128
_SUBLANES = 8
_VMEM_LIMIT = 56 * 1024 * 1024

_F32 = jnp.float32
_BF16 = jnp.bfloat16
_NT = (((1,), (1,)), ((), ()))


def _bf(x):
    return x.astype(_BF16)


def _dot(a, b):
    return jnp.dot(a, b, preferred_element_type=_F32)


def _dot_nt(a, b):
    return lax.dot_general(a, b, _NT, preferred_element_type=_F32)


def _rms(x):
    return x * lax.rsqrt(jnp.mean(x * x, axis=-1, keepdims=True) + _EPS)


def _modnorm(x, nw, scale, shift):
    return (_rms(x) * nw) * (1.0 + scale) + shift


def _silu(x):
    return x * jax.nn.sigmoid(x)


def _neg_expm1(x):
    u = jnp.exp(x)
    lu = jnp.log(u)
    safe = jnp.where((u == 1.0) | (u == 0.0), 1.0, lu)
    em1 = jnp.where(u == 1.0, x, jnp.where(u == 0.0, -1.0, (u - 1.0) * x / safe))
    return -em1


def _softplus(z):
    return jnp.maximum(z, 0.0) + jnp.log1p(jnp.exp(-jnp.abs(z)))


def _full_spec(shape):
    nd = len(shape)
    return pl.BlockSpec(shape, lambda *_: (0,) * nd)


def _params(sem):
    return pltpu.CompilerParams(dimension_semantics=sem, vmem_limit_bytes=_VMEM_LIMIT)


def _adaln_kernel(c_ref, w_ref, b_ref, o_ref):
    o_ref[...] = _dot(_bf(_silu(c_ref[...])), _bf(w_ref[...])) + b_ref[...]


def _adaln(c, w, b):
    rows, d = c.shape
    n = w.shape[1]
    tn = d
    return pl.pallas_call(
        _adaln_kernel,
        grid=(n // tn,),
        in_specs=[pl.BlockSpec((rows, d), lambda j: (0, 0)),
                  pl.BlockSpec((d, tn), lambda j: (0, j)),
                  pl.BlockSpec((1, tn), lambda j: (0, j))],
        out_specs=pl.BlockSpec((rows, tn), lambda j: (0, j)),
        out_shape=jax.ShapeDtypeStruct((rows, n), _F32),
        compiler_params=_params(("arbitrary",)),
        name="adaln",
    )(c, w, b.reshape(1, n))


def _rope_table_kernel(inv_ref, cos_ref, sin_ref, *, pos0):
    half, n = cos_ref.shape
    pos = (lax.broadcasted_iota(jnp.int32, (half, n), 1) + pos0).astype(_F32)
    ang = pos * inv_ref[...]
    cos_ref[...] = jnp.cos(ang)
    sin_ref[...] = jnp.sin(ang)


def _rope_tables_t(half, n, pos0):
    inv = np.power(_ROPE_BASE, -np.arange(half, dtype=np.float32) / half).astype(np.float32)
    return pl.pallas_call(
        functools.partial(_rope_table_kernel, pos0=pos0),
        out_shape=(jax.ShapeDtypeStruct((half, n), _F32),) * 2,
        name="rope_tables",
    )(jnp.asarray(inv).reshape(half, 1))


def _row_tables(cos_t, sin_t, reps, width):
    def one(t):
        r = jnp.concatenate([t.T, t.T], axis=1)
        r = jnp.tile(r, (1, reps))
        return jnp.pad(r, ((0, 0), (0, width - r.shape[1])))
    return one(cos_t), one(sin_t)


def _kpe_rope(t8, tabc, tabs, rope):
    return t8 * tabc + pltpu.roll(t8, _LANES - rope, 1) * tabs


def _conv_gates(u, shifted, convw_ref, convb_ref, bda_ref, ba_ref, bdx_ref, bx_ref, lam_ref):
    rows, width = u.shape
    conv_w = convw_ref.shape[0]
    acc = convb_ref[...]
    for k in range(conv_w):
        d = conv_w - 1 - k
        acc = acc + (u if d == 0 else shifted(d)) * convw_ref[k:k + 1, :]
    uc = acc
    ucb = _bf(uc)
    half = width // 2
    ra = jnp.concatenate([_dot(ucb[:, :half], bda_ref[0]), _dot(ucb[:, half:], bda_ref[1])], axis=1)
    rx = jnp.concatenate([_dot(ucb[:, :half], bdx_ref[0]), _dot(ucb[:, half:], bdx_ref[1])], axis=1)
    r = jax.nn.sigmoid(ra + ba_ref[...])
    i = jax.nn.sigmoid(rx + bx_ref[...])
    log_a = -_LRU_C * r * _softplus(-lam_ref[...])
    a = jnp.exp(log_a)
    b = jnp.sqrt(_neg_expm1(2.0 * log_a)) * i * uc
    return a, b


def _group_scan(a, b):
    rin = lax.broadcasted_iota(jnp.int32, a.shape, 0) % _SUBLANES
    d = 1
    while d < _SUBLANES:
        ok = rin >= d
        b = jnp.where(ok, a * pltpu.roll(b, d, 0) + b, b)
        a = jnp.where(ok, a * pltpu.roll(a, d, 0), a)
        d *= 2
    return a, b


def _prompt_proj_kernel(x_ref, shift_ref, scale_ref, nw_ref, win_ref, qnw_ref, kvnw_ref,
                        wuqt_ref, wukp_ref, wuvt_ref, eplace_ref, cost_ref, sint_ref, tabc_ref, tabs_ref,
                        convw_ref, convb_ref, bda_ref, ba_ref, bdx_ref, bx_ref, lam_ref,
                        qt_ref, k_ref, vt_ref, ckv_ref, kpe_ref, lg_ref, ga_ref, utail_ref, htail_ref,
                        uprev_scr, hprev_scr, *, dims, qscale):
    q_lora, kv_lora, lru_w, n_heads, nope, rope, v_dim = dims
    tm = x_ref.shape[1]
    half = rope // 2
    hq = _LANES

    @pl.when(pl.program_id(1) == 0)
    def _():
        uprev_scr[...] = jnp.zeros_like(uprev_scr)
        hprev_scr[...] = jnp.zeros_like(hprev_scr)

    h = _modnorm(x_ref[0], nw_ref[...], scale_ref[...], shift_ref[...])
    proj = _dot(_bf(h), win_ref[...])
    o = 0
    q_lat = proj[:, o:o + q_lora]; o += q_lora
    kv_lat = proj[:, o:o + kv_lora]; o += kv_lora
    u = proj[:, o:o + lru_w]; o += lru_w
    g_b = proj[:, o:o + lru_w]; o += lru_w
    g_a = proj[:, o:o + n_heads * v_dim]; o += n_heads * v_dim
    t8 = proj[:, o:o + _LANES]

    qn = _rms(q_lat) * qnw_ref[...]
    qt = _dot_nt(wuqt_ref[...], _bf(qn)) * qscale
    cos_t = cost_ref[...]
    sin_t = sint_ref[...]
    for hd in range(n_heads):
        base = hd * hq
        x1 = qt[base + nope:base + nope + half]
        x2 = qt[base + nope + half:base + nope + rope]
        blk = jnp.concatenate([qt[base:base + nope], x1 * cos_t - x2 * sin_t, x2 * cos_t + x1 * sin_t,
                               qt[base + nope + rope:base + hq]], axis=0)
        qt_ref[0, hd] = _bf(blk)

    ckv = _rms(kv_lat) * kvnw_ref[...]
    ckv_ref[0] = ckv
    kpe = _kpe_rope(t8, tabc_ref[...], tabs_ref[...], rope)
    kpe_ref[0] = kpe[:, :rope]
    ckv_b = _bf(ckv)
    k_all = _dot(ckv_b, wukp_ref[...]) + _dot(_bf(kpe), eplace_ref[...])
    for hd in range(n_heads):
        k_ref[0, hd] = _bf(k_all[:, hd * hq:(hd + 1) * hq])
    vt = _dot_nt(wuvt_ref[...], ckv_b)
    for hd in range(n_heads):
        vt_ref[0, hd] = _bf(vt[hd * v_dim:(hd + 1) * v_dim])
    ga_ref[0] = _bf(_silu(g_a))

    uprev = uprev_scr[...]
    rin8 = lax.broadcasted_iota(jnp.int32, uprev.shape, 0)

    def shifted(d):
        ru = pltpu.roll(u, d, 0)
        top = jnp.where(rin8 < d, pltpu.roll(uprev, d, 0), ru[:_SUBLANES])
        return jnp.concatenate([top, ru[_SUBLANES:]], axis=0)

    a, b = _conv_gates(u, shifted, convw_ref, convb_ref, bda_ref, ba_ref, bdx_ref, bx_ref, lam_ref)
    a, b = _group_scan(a, b)
    hcar = hprev_scr[_SUBLANES - 1:_SUBLANES, :]
    outs = []
    for g in range(tm // _SUBLANES):
        hg = a[g * _SUBLANES:(g + 1) * _SUBLANES] * hcar + b[g * _SUBLANES:(g + 1) * _SUBLANES]
        outs.append(hg)
        hcar = hg[_SUBLANES - 1:_SUBLANES, :]
    hall = jnp.concatenate(outs, axis=0)
    lg_ref[0] = _bf(hall * _silu(g_b))
    utail = u[tm - _SUBLANES:, :]
    htail = hall[tm - _SUBLANES:, :]
    uprev_scr[...] = utail
    hprev_scr[...] = htail
    utail_ref[0] = utail
    htail_ref[0] = htail


def _prompt_proj(x, shift, scale, wts, tabs, dims, qscale, tm):
    bsz, s, d = x.shape
    q_lora, kv_lora, lru_w, n_heads, nope, rope, v_dim = dims
    cos_t, sin_t, tabc, tabs_ = tabs
    half = rope // 2
    nt = s // tm
    consts = [wts["nw"], wts["win_b"], wts["qnw"], wts["kvnw"], wts["wuqt"], wts["wukp"], wts["wuvt"],
              wts["eplace"]]
    lru_consts = [wts["convw"], wts["convb"], wts["bda"], wts["ba"], wts["bdx"], wts["bx"], wts["lam"]]
    in_specs = ([pl.BlockSpec((1, tm, d), lambda b, i: (b, i, 0)),
                 pl.BlockSpec((None, 1, d), lambda b, i: (b, 0, 0)),
                 pl.BlockSpec((None, 1, d), lambda b, i: (b, 0, 0))]
                + [_full_spec(c.shape) for c in consts]
                + [pl.BlockSpec((half, tm), lambda b, i: (0, i)),
                   pl.BlockSpec((half, tm), lambda b, i: (0, i)),
                   pl.BlockSpec((tm, _LANES), lambda b, i: (i, 0)),
                   pl.BlockSpec((tm, _LANES), lambda b, i: (i, 0))]
                + [_full_spec(c.shape) for c in lru_consts])
    out_shape = (
        jax.ShapeDtypeStruct((bsz, n_heads, _LANES, s), _BF16),
        jax.ShapeDtypeStruct((bsz, n_heads, s, _LANES), _BF16),
        jax.ShapeDtypeStruct((bsz, n_heads, v_dim, s), _BF16),
        jax.ShapeDtypeStruct((bsz, s, kv_lora), _F32),
        jax.ShapeDtypeStruct((bsz, s, rope), _F32),
        jax.ShapeDtypeStruct((bsz, s, lru_w), _BF16),
        jax.ShapeDtypeStruct((bsz, s, n_heads * v_dim), _BF16),
        jax.ShapeDtypeStruct((bsz, _SUBLANES, lru_w), _F32),
        jax.ShapeDtypeStruct((bsz, _SUBLANES, lru_w), _F32),
    )
    out_specs = (
        pl.BlockSpec((1, n_heads, _LANES, tm), lambda b, i: (b, 0, 0, i)),
        pl.BlockSpec((1, n_heads, tm, _LANES), lambda b, i: (b, 0, i, 0)),
        pl.BlockSpec((1, n_heads, v_dim, tm), lambda b, i: (b, 0, 0, i)),
        pl.BlockSpec((1, tm, kv_lora), lambda b, i: (b, i, 0)),
        pl.BlockSpec((1, tm, rope), lambda b, i: (b, i, 0)),
        pl.BlockSpec((1, tm, lru_w), lambda b, i: (b, i, 0)),
        pl.BlockSpec((1, tm, n_heads * v_dim), lambda b, i: (b, i, 0)),
        pl.BlockSpec((1, _SUBLANES, lru_w), lambda b, i: (b, 0, 0)),
        pl.BlockSpec((1, _SUBLANES, lru_w), lambda b, i: (b, 0, 0)),
    )
    return pl.pallas_call(
        functools.partial(_prompt_proj_kernel, dims=dims, qscale=qscale),
        grid=(bsz, nt),
        in_specs=in_specs,
        out_specs=out_specs,
        out_shape=out_shape,
        scratch_shapes=[pltpu.VMEM((_SUBLANES, lru_w), _F32), pltpu.VMEM((_SUBLANES, lru_w), _F32)],
        compiler_params=_params(("arbitrary", "arbitrary")),
        name="prompt_proj",
    )(x, shift, scale, *consts, cos_t, sin_t, tabc, tabs_, *lru_consts)


def _prompt_attn_kernel(qi_ref, ki_ref, qt_ref, k_ref, vt_ref, ga_ref, o_ref, m_scr, l_scr, acc_scr,
                        *, n_heads, v_dim):
    step = pl.program_id(1)
    qi = qi_ref[step]
    ki = ki_ref[step]
    tk = k_ref.shape[2]
    tq = qt_ref.shape[3]

    @pl.when(ki == 0)
    def _():
        m_scr[...] = jnp.full_like(m_scr, _NEG)
        l_scr[...] = jnp.zeros_like(l_scr)
        acc_scr[...] = jnp.zeros_like(acc_scr)

    def sweep(masked):
        if masked:
            keep = (lax.broadcasted_iota(jnp.int32, (tk, tq), 0)
                    <= lax.broadcasted_iota(jnp.int32, (tk, tq), 1))

        def head(hd, carry):
            st = _dot(k_ref[0, hd], qt_ref[0, hd])
            if masked:
                st = jnp.where(keep, st, _NEG)
            m_prev = m_scr[hd]
            m_new = jnp.maximum(m_prev, jnp.max(st, axis=0, keepdims=True))
            alpha = jnp.exp2(m_prev - m_new)
            p = jnp.exp2(st - m_new)
            l_scr[hd] = alpha * l_scr[hd] + jnp.sum(p, axis=0, keepdims=True)
            m_scr[hd] = m_new
            rows = pl.ds(pl.multiple_of(hd * v_dim, v_dim), v_dim)
            acc_scr[rows, :] = alpha * acc_scr[rows, :] + _dot(vt_ref[0, hd], _bf(p))
            return carry

        lax.fori_loop(0, n_heads, head, 0)

    @pl.when(ki != qi)
    def _():
        sweep(False)

    @pl.when(ki == qi)
    def _():
        sweep(True)
        parts = [acc_scr[hd * v_dim:(hd + 1) * v_dim, :] * (1.0 / l_scr[hd]) for hd in range(n_heads)]
        o = jnp.concatenate(parts, axis=0).T
        o_ref[0] = _bf(o * ga_ref[0].astype(_F32))


def _prompt_attn(qt, k, vt, ga, tq):
    bsz, n_heads, hq, s = qt.shape
    v_dim = vt.shape[2]
    nq = s // tq
    qi = np.concatenate([np.full(i + 1, i, np.int32) for i in range(nq)])
    ki = np.concatenate([np.arange(i + 1, dtype=np.int32) for i in range(nq)])
    grid_spec = pltpu.PrefetchScalarGridSpec(
        num_scalar_prefetch=2,
        grid=(bsz, len(qi)),
        in_specs=[pl.BlockSpec((1, n_heads, hq, tq), lambda b, p, qi, ki: (b, 0, 0, qi[p])),
                  pl.BlockSpec((1, n_heads, tq, hq), lambda b, p, qi, ki: (b, 0, ki[p], 0)),
                  pl.BlockSpec((1, n_heads, v_dim, tq), lambda b, p, qi, ki: (b, 0, 0, ki[p])),
                  pl.BlockSpec((1, tq, n_heads * v_dim), lambda b, p, qi, ki: (b, qi[p], 0))],
        out_specs=pl.BlockSpec((1, tq, n_heads * v_dim), lambda b, p, qi, ki: (b, qi[p], 0)),
        scratch_shapes=[pltpu.VMEM((n_heads, 1, tq), _F32), pltpu.VMEM((n_heads, 1, tq), _F32),
                        pltpu.VMEM((n_heads * v_dim, tq), _F32)],
    )
    return pl.pallas_call(
        functools.partial(_prompt_attn_kernel, n_heads=n_heads, v_dim=v_dim),
        grid_spec=grid_spec,
        out_shape=jax.ShapeDtypeStruct((bsz, s, n_heads * v_dim), _BF16),
        compiler_params=_params(("arbitrary", "arbitrary")),
        name="prompt_attn",
    )(jnp.asarray(qi), jnp.asarray(ki), qt, k, vt, ga)


def _post_kernel(x_ref, shift_ref, scale_ref, gate_ref, nw_ref, ag_ref, lg_ref, wm_ref, woa_ref, wob_ref,
                 wout_ref, fnw_ref, o_ref, *, final_norm):
    x = x_ref[...]
    d = x.shape[1]
    h = _modnorm(x, nw_ref[...], scale_ref[...], shift_ref[...])
    m = _dot(_bf(h), wm_ref[...])
    ya = _dot(_bf(ag_ref[...]), woa_ref[...])
    yb = _dot(_bf(lg_ref[...]), wob_ref[...])
    merged = jax.nn.sigmoid(m[:, :d]) * ya + jax.nn.sigmoid(m[:, d:]) * yb
    xo = x + gate_ref[...] * _dot(_bf(merged), wout_ref[...])
    if final_norm:
        xo = _rms(xo) * fnw_ref[...]
    o_ref[...] = xo


def _post(x2, mods, ag, lg, wts, tm, rows_per_mod, final_norm):
    rows, d = x2.shape
    if rows_per_mod is None:
        mod_spec = pl.BlockSpec((tm, d), lambda i: (i, 0))
    else:
        tiles = rows_per_mod // tm
        mod_spec = pl.BlockSpec((None, 1, d), lambda i: (i // tiles, 0, 0))
    consts = [wts["win_d"], wts["woa"], wts["wob"], wts["wout"], wts["fnw"]]
    return pl.pallas_call(
        functools.partial(_post_kernel, final_norm=final_norm),
        grid=(rows // tm,),
        in_specs=[pl.BlockSpec((tm, d), lambda i: (i, 0)), mod_spec, mod_spec, mod_spec,
                  _full_spec(wts["nw"].shape),
                  pl.BlockSpec((tm, ag.shape[1]), lambda i: (i, 0)),
                  pl.BlockSpec((tm, lg.shape[1]), lambda i: (i, 0))]
                 + [_full_spec(c.shape) for c in consts],
        out_specs=pl.BlockSpec((tm, d), lambda i: (i, 0)),
        out_shape=jax.ShapeDtypeStruct((rows, d), _F32),
        compiler_params=_params(("arbitrary",)),
        name="post",
    )(x2, *mods, wts["nw"], ag, lg, *consts)


def _sample_proj_kernel(x_ref, shift_ref, scale_ref, nw_ref, win_ref, qnw_ref, kvnw_ref,
                        wuqn_ref, wuqp_ref, wuqs_ref, wukt_ref, qcos_ref, qsin_ref, tabc_ref, tabs_ref,
                        cprev_ref, h0_ref,
                        convw_ref, convb_ref, bda_ref, ba_ref, bdx_ref, bx_ref, lam_ref,
                        qa_ref, qp_ref, ckv_ref, kpe_ref, lg_ref, ga_ref, u_ref, h_ref, *, dims, qscale):
    q_lora, kv_lora, lru_w, n_heads, nope, rope, v_dim = dims
    tm = x_ref.shape[0]

    h = _modnorm(x_ref[...], nw_ref[...], scale_ref[...], shift_ref[...])
    proj = _dot(_bf(h), win_ref[...])
    o = 0
    q_lat = proj[:, o:o + q_lora]; o += q_lora
    kv_lat = proj[:, o:o + kv_lora]; o += kv_lora
    u = proj[:, o:o + lru_w]; o += lru_w
    g_b = proj[:, o:o + lru_w]; o += lru_w
    g_a = proj[:, o:o + n_heads * v_dim]; o += n_heads * v_dim
    t8 = proj[:, o:o + _LANES]

    qn = _bf(_rms(q_lat) * qnw_ref[...])
    qa = [_dot(_bf(_dot(qn, wuqn_ref[hd])), wukt_ref[hd]) for hd in range(n_heads)]
    qa_ref[...] = jnp.concatenate(qa, axis=1) * qscale
    qp_ref[...] = (_dot(qn, wuqp_ref[...]) * qcos_ref[...] + _dot(qn, wuqs_ref[...]) * qsin_ref[...]) * qscale

    ckv_ref[...] = _rms(kv_lat) * kvnw_ref[...]
    kpe_ref[...] = _kpe_rope(t8, tabc_ref[...], tabs_ref[...], rope)[:, :rope]
    ga_ref[...] = _silu(g_a)

    cprev = cprev_ref[...]
    rin = lax.broadcasted_iota(jnp.int32, u.shape, 0) % _SUBLANES

    def shifted(d):
        return jnp.where(rin < d, pltpu.roll(cprev, tm - _SUBLANES + d, 0), pltpu.roll(u, d, 0))

    a, b = _conv_gates(u, shifted, convw_ref, convb_ref, bda_ref, ba_ref, bdx_ref, bx_ref, lam_ref)
    a, b = _group_scan(a, b)
    hall = a * h0_ref[...] + b
    lg_ref[...] = hall * _silu(g_b)
    u_ref[...] = u
    h_ref[...] = hall


def _sample_proj(x2, shift, scale, cprev, h0, wts, tabs, dims, qscale, tm):
    rows, d = x2.shape
    q_lora, kv_lora, lru_w, n_heads, nope, rope, v_dim = dims
    qcos, qsin, tabc, tabs_ = tabs
    consts = [wts["nw"], wts["win_b"], wts["qnw"], wts["kvnw"], wts["wuqn"], wts["wuqp"], wts["wuqs"],
              wts["wukt"], qcos, qsin, tabc, tabs_]
    lru_consts = [wts["convw"], wts["convb"], wts["bda"], wts["ba"], wts["bdx"], wts["bx"], wts["lam"]]

    def row_spec(w):
        return pl.BlockSpec((tm, w), lambda i: (i, 0))

    widths = (n_heads * kv_lora, n_heads * _LANES, kv_lora, rope, lru_w, n_heads * v_dim, lru_w, lru_w)
    return pl.pallas_call(
        functools.partial(_sample_proj_kernel, dims=dims, qscale=qscale),
        grid=(rows // tm,),
        in_specs=[row_spec(d), row_spec(d), row_spec(d)] + [_full_spec(c.shape) for c in consts]
                 + [row_spec(lru_w), row_spec(lru_w)] + [_full_spec(c.shape) for c in lru_consts],
        out_specs=tuple(row_spec(w) for w in widths),
        out_shape=tuple(jax.ShapeDtypeStruct((rows, w), _F32) for w in widths),
        compiler_params=_params(("arbitrary",)),
        name="sample_proj",
    )(x2, shift, scale, *consts, cprev, h0, *lru_consts)


def _sample_attn_kernel(pt_ref, qa_ref, qp_ref, cnew_ref, rnew_ref, ga_ref, wuv_ref, ckv_hbm, kpe_hbm,
                        o_ref, cbuf, rbuf, sem, qa_scr, qp_scr, m_scr, l_scr, acc_scr,
                        *, layer, pages_per_step, page, n_heads, t_new, v_dim):
    b = pl.program_id(0)
    j = pl.program_id(1)
    nb = pl.num_programs(0)
    nch = pl.num_programs(1)
    step = b * nch + j
    slot = lax.rem(step, 2)
    kv_lora = cbuf.shape[2]
    rope = rbuf.shape[2]
    nq = n_heads * t_new

    def copies(bb, jj, sl):
        out = []
        for i in range(pages_per_step):
            pg = pt_ref[bb, jj * pages_per_step + i]
            rows = pl.ds(i * page, page)
            out.append(pltpu.make_async_copy(ckv_hbm.at[layer, pg], cbuf.at[sl, rows], sem.at[0, sl]))
            out.append(pltpu.make_async_copy(kpe_hbm.at[layer, pg], rbuf.at[sl, rows], sem.at[1, sl]))
        return out

    @pl.when(step == 0)
    def _():
        for c in copies(0, 0, 0):
            c.start()

    @pl.when(step + 1 < nb * nch)
    def _():
        wrap = j + 1 == nch
        for c in copies(jnp.where(wrap, b + 1, b), jnp.where(wrap, 0, j + 1), 1 - slot):
            c.start()

    @pl.when(j == 0)
    def _():
        qa = qa_ref[...]
        qp = qp_ref[...]
        qa_scr[...] = _bf(jnp.concatenate(
            [qa[:, hd * kv_lora:(hd + 1) * kv_lora] for hd in range(n_heads)], axis=0))
        qp_scr[...] = _bf(jnp.concatenate(
            [qp[:, hd * _LANES:hd * _LANES + rope] for hd in range(n_heads)], axis=0))
        m_scr[...] = jnp.full_like(m_scr, _NEG)
        l_scr[...] = jnp.zeros_like(l_scr)
        acc_scr[...] = jnp.zeros_like(acc_scr)

    for c in copies(b, j, slot):
        c.wait()

    def online(s, vals):
        m_prev = m_scr[...]
        m_new = jnp.maximum(m_prev, jnp.max(s, axis=1, keepdims=True))
        alpha = jnp.exp2(m_prev - m_new)
        p = jnp.exp2(s - m_new)
        l_scr[...] = alpha * l_scr[...] + jnp.sum(p, axis=1, keepdims=True)
        m_scr[...] = m_new
        acc_scr[...] = alpha * acc_scr[...] + _dot(_bf(p), vals)

    cb = _bf(cbuf[slot])
    rb = _bf(rbuf[slot])
    online(_dot_nt(qa_scr[...], cb) + _dot_nt(qp_scr[...], rb), cb)

    @pl.when(j == nch - 1)
    def _():
        pad = jnp.zeros((_LANES - t_new, kv_lora), _F32)
        cn = _bf(jnp.concatenate([cnew_ref[...], pad], axis=0))
        rn = _bf(jnp.concatenate([rnew_ref[...], pad[:, :rope]], axis=0))
        s = _dot_nt(qa_scr[...], cn) + _dot_nt(qp_scr[...], rn)
        col = lax.broadcasted_iota(jnp.int32, s.shape, 1)
        trow = lax.broadcasted_iota(jnp.int32, s.shape, 0) % t_new
        online(jnp.where(col <= trow, s, _NEG), cn)
        o_lat = _bf(acc_scr[...] * (1.0 / l_scr[...]))
        o_all = _dot(o_lat, wuv_ref[...])
        lane_head = lax.broadcasted_iota(jnp.int32, (t_new, n_heads * v_dim), 1) // v_dim
        o = jnp.zeros((t_new, n_heads * v_dim), _F32)
        for hd in range(n_heads):
            o = o + jnp.where(lane_head == hd, o_all[hd * t_new:(hd + 1) * t_new], 0.0)
        o_ref[...] = o * ga_ref[...]


def _sample_attn(page_table, qa, qp, cnew, rnew, ga, wuv, cache_ckv, cache_kpe, layer, dims, t_new,
                 pages_per_step):
    q_lora, kv_lora, lru_w, n_heads, nope, rope, v_dim = dims
    nb, n_pages = page_table.shape
    page = cache_ckv.shape[2]
    nch = n_pages // pages_per_step
    chunk = pages_per_step * page
    nq = n_heads * t_new

    def row_spec(w):
        return pl.BlockSpec((t_new, w), lambda b, j, pt: (b, 0))

    grid_spec = pltpu.PrefetchScalarGridSpec(
        num_scalar_prefetch=1,
        grid=(nb, nch),
        in_specs=[row_spec(n_heads * kv_lora), row_spec(n_heads * _LANES), row_spec(kv_lora), row_spec(rope),
                  row_spec(n_heads * v_dim),
                  pl.BlockSpec(wuv.shape, lambda b, j, pt: (0, 0)),
                  pl.BlockSpec(memory_space=pl.ANY), pl.BlockSpec(memory_space=pl.ANY)],
        out_specs=row_spec(n_heads * v_dim),
        scratch_shapes=[pltpu.VMEM((2, chunk, kv_lora), _F32), pltpu.VMEM((2, chunk, rope), _F32),
                        pltpu.SemaphoreType.DMA((2, 2)),
                        pltpu.VMEM((nq, kv_lora), _BF16), pltpu.VMEM((nq, rope), _BF16),
                        pltpu.VMEM((nq, 1), _F32), pltpu.VMEM((nq, 1), _F32), pltpu.VMEM((nq, kv_lora), _F32)],
    )
    return pl.pallas_call(
        functools.partial(_sample_attn_kernel, layer=layer, pages_per_step=pages_per_step, page=page,
                          n_heads=n_heads, t_new=t_new, v_dim=v_dim),
        grid_spec=grid_spec,
        out_shape=jax.ShapeDtypeStruct((nb * t_new, n_heads * v_dim), _F32),
        compiler_params=_params(("arbitrary", "arbitrary")),
        name="sample_attn",
    )(page_table, qa, qp, cnew, rnew, ga, wuv, cache_ckv, cache_kpe)


def _block_diag_halves(w):
    nb, n, _ = w.shape
    hb = nb // 2
    out = jnp.zeros((2, hb * n, hb * n), w.dtype)
    for i in range(nb):
        r = (i % hb) * n
        out = out.at[i // hb, r:r + n, r:r + n].set(w[i])
    return _bf(out)


def _prep_weights(l, dims, ada_w, ada_b, norm_w, w_in, q_norm_w, kv_norm_w, w_uq, w_uk, w_uv, conv_w, conv_b,
                  lru_wa, lru_ba, lru_wx, lru_bx, lru_lambda, w_oa, w_ob, w_out, final_norm_w):
    q_lora, kv_lora, lru_w, n_heads, nope, rope, v_dim = dims
    d = w_in.shape[1]
    half = rope // 2
    mla_w = n_heads * v_dim
    widths = (q_lora, kv_lora, rope, mla_w, lru_w, lru_w, d, d)
    offs = np.concatenate([[0], np.cumsum(widths)])
    cols = [w_in[l][:, offs[i]:offs[i + 1]] for i in range(len(widths))]
    c_q, c_kv, c_kpe, c_ga, c_u, c_gb, c_ma, c_mb = cols
    kpe_sw = jnp.concatenate([-c_kpe[:, half:], c_kpe[:, :half]], axis=1)
    kpe_tile = jnp.pad(jnp.concatenate([c_kpe, kpe_sw], axis=1), ((0, 0), (0, _LANES - 2 * rope)))
    wq = w_uq[l]
    wq_n, wq_p = wq[:, :, :nope], wq[:, :, nope:]
    wq_s = jnp.concatenate([-wq_p[:, :, half:], wq_p[:, :, :half]], axis=2)
    pad_q = _LANES - nope - rope
    wuqt = jnp.pad(wq, ((0, 0), (0, 0), (0, pad_q))).reshape(q_lora, n_heads * _LANES).T
    eplace = np.zeros((_LANES, n_heads * _LANES), np.float32)
    for hd in range(n_heads):
        eplace[np.arange(rope), hd * _LANES + nope + np.arange(rope)] = 1.0

    def pad_heads(w):
        return jnp.pad(w, ((0, 0), (0, 0), (0, _LANES - rope))).reshape(q_lora, n_heads * _LANES)

    row = lambda v: v.reshape(1, -1)
    return {
        "ada_w": ada_w[l], "ada_b": ada_b[l],
        "nw": row(norm_w[l]), "qnw": row(q_norm_w[l]), "kvnw": row(kv_norm_w[l]), "fnw": row(final_norm_w),
        "win_b": _bf(jnp.concatenate([c_q, c_kv, c_u, c_gb, c_ga, kpe_tile], axis=1)),
        "win_d": _bf(jnp.concatenate([c_ma, c_mb], axis=1)),
        "wuqt": _bf(wuqt),
        "wukp": _bf(jnp.pad(w_uk[l], ((0, 0), (0, 0), (0, _LANES - nope))).reshape(kv_lora, n_heads * _LANES)),
        "wuvt": _bf(w_uv[l].reshape(kv_lora, mla_w).T),
        "eplace": _bf(jnp.asarray(eplace)),
        "wuqn": _bf(jnp.transpose(wq_n, (1, 0, 2))),
        "wuqp": _bf(pad_heads(wq_p)), "wuqs": _bf(pad_heads(wq_s)),
        "wukt": _bf(jnp.transpose(w_uk[l], (1, 2, 0))),
        "wuv": _bf(w_uv[l].reshape(kv_lora, mla_w)),
        "convw": conv_w[l], "convb": row(conv_b[l]),
        "bda": _block_diag_halves(lru_wa[l]), "ba": row(lru_ba[l]),
        "bdx": _block_diag_halves(lru_wx[l]), "bx": row(lru_bx[l]), "lam": row(lru_lambda[l]),
        "woa": _bf(w_oa[l]), "wob": _bf(w_ob[l]), "wout": _bf(w_out[l]),
    }


def kernel(x_prompt, x_sample, c_prompt, c_sample, cache_ckv, cache_kpe, state_conv, state_lru, page_table, ada_w, ada_b, norm_w, w_in, q_norm_w, kv_norm_w, w_uq, w_uk, w_uv, conv_w, conv_b, lru_wa, lru_ba, lru_wx, lru_bx, lru_lambda, w_oa, w_ob, w_out, final_norm_w):
    bsz, s, d = x_prompt.shape
    dbsz, t_new, _ = x_sample.shape
    depth = w_in.shape[0]
    q_lora, n_heads, qk = w_uq.shape[1:]
    kv_lora, _, nope = w_uk.shape[1:]
    v_dim = w_uv.shape[3]
    rope = qk - nope
    half = rope // 2
    lru_w = conv_w.shape[2]
    conv_taps = conv_w.shape[1]
    n_pages, page = page_table.shape[1], cache_ckv.shape[2]
    dims = (q_lora, kv_lora, lru_w, n_heads, nope, rope, v_dim)
    assert t_new == _SUBLANES and conv_taps - 1 <= _SUBLANES and nope + rope <= _LANES
    qscale = _LOG2E / math.sqrt(qk)

    tm = min(512, s)
    tq = min(512, s)
    ts = min(256, dbsz * t_new)
    pages_per_step = min(16, n_pages)

    cos_p, sin_p = _rope_tables_t(half, s, 0)
    tabc_p, tabs_p = _row_tables(cos_p, sin_p, 1, _LANES)
    cos_s, sin_s = _rope_tables_t(half, _LANES, n_pages * page)
    cos_s, sin_s = cos_s[:, :t_new], sin_s[:, :t_new]
    tabc_s, tabs_s = _row_tables(cos_s, sin_s, 1, _LANES)
    qcos_s, qsin_s = _row_tables(cos_s, sin_s, 1, _LANES)
    qcos_s, qsin_s = jnp.tile(qcos_s, (1, n_heads)), jnp.tile(qsin_s, (1, n_heads))
    tabc_s, tabs_s, qcos_s, qsin_s = (jnp.tile(t, (ts // t_new, 1)) for t in (tabc_s, tabs_s, qcos_s, qsin_s))

    xp, xs = x_prompt, x_sample
    outs = [[] for _ in range(8)]
    c_all = jnp.concatenate([c_prompt, c_sample], axis=0)
    c_all = jnp.pad(c_all, ((0, (-c_all.shape[0]) % _SUBLANES), (0, 0)))
    for l in range(depth):
        w = _prep_weights(l, dims, ada_w, ada_b, norm_w, w_in, q_norm_w, kv_norm_w, w_uq, w_uk, w_uv, conv_w,
                          conv_b, lru_wa, lru_ba, lru_wx, lru_bx, lru_lambda, w_oa, w_ob, w_out, final_norm_w)
        last = l == depth - 1
        mod = _adaln(c_all, w["ada_w"], w["ada_b"])
        mod_p = [mod[:bsz, i * d:(i + 1) * d].reshape(bsz, 1, d) for i in range(3)]
        mod_s = [jnp.repeat(mod[bsz:bsz + dbsz, i * d:(i + 1) * d], t_new, axis=0) for i in range(3)]

        qt, k, vt, ckv_p, kpe_p, lg_p, ga_p, utail, htail = _prompt_proj(
            xp, mod_p[0], mod_p[1], w, (cos_p, sin_p, tabc_p, tabs_p), dims, qscale, tm)
        ag_p = _prompt_attn(qt, k, vt, ga_p, tq)
        xp = _post(xp.reshape(bsz * s, d), mod_p, ag_p.reshape(bsz * s, -1), lg_p.reshape(bsz * s, -1), w,
                   tm, s, last).reshape(bsz, s, d)
        outs[0].append(ckv_p); outs[1].append(kpe_p)
        outs[2].append(utail[:, _SUBLANES - (conv_taps - 1):]); outs[3].append(htail[:, _SUBLANES - 1])

        cprev = jnp.pad(state_conv[l], ((0, 0), (_SUBLANES - (conv_taps - 1), 0), (0, 0)))
        h0 = jnp.repeat(state_lru[l], t_new, axis=0)
        xs2 = xs.reshape(dbsz * t_new, d)
        qa, qp, ckv_s, kpe_s, lg_s, ga_s, u_s, h_s = _sample_proj(
            xs2, mod_s[0], mod_s[1], cprev.reshape(dbsz * t_new, lru_w), h0, w,
            (qcos_s, qsin_s, tabc_s, tabs_s), dims, qscale, ts)
        ag_s = _sample_attn(page_table, qa, qp, ckv_s, kpe_s, ga_s, w["wuv"], cache_ckv, cache_kpe, l, dims,
                            t_new, pages_per_step)
        xs = _post(xs2, mod_s, ag_s, lg_s, w, ts, None, last).reshape(dbsz, t_new, d)
        outs[4].append(ckv_s.reshape(dbsz, t_new, kv_lora)); outs[5].append(kpe_s.reshape(dbsz, t_new, rope))
        outs[6].append(u_s.reshape(dbsz, t_new, lru_w)[:, t_new - (conv_taps - 1):])
        outs[7].append(h_s.reshape(dbsz, t_new, lru_w)[:, t_new - 1])

    return (xp, xs) + tuple(jnp.stack(o) for o in outs)
```
